```python
import jax, jax.numpy as jnp
from jax import lax
import numpy as np

D_MODEL = 1024
BATCH = 32
SEQ = 2048
DEPTH = 1

D_MIX = D_MODEL
D_A = D_MIX // 2
D_B = D_MIX - D_A
N_HEADS_A = 4
HEAD_DIM_A = D_A // N_HEADS_A
CHUNK = 128
N_GROUPS_B = 4
GROUP_DIM_B = D_B // N_GROUPS_B
D_IN = 3 * D_A + 2 * D_B
EPS = 1e-6

kernel_name = "hybrid_gmlp_fnet_encoder_layer"


def rms_norm(x, g):
    xf = x.astype(jnp.float32)
    ms = jnp.mean(xf * xf, axis=-1, keepdims=True)
    return (xf * lax.rsqrt(ms + EPS)).astype(x.dtype) * g


def layer_norm(x, g, b):
    xf = x.astype(jnp.float32)
    mu = jnp.mean(xf, axis=-1, keepdims=True)
    var = jnp.mean(jnp.square(xf - mu), axis=-1, keepdims=True)
    return ((xf - mu) * lax.rsqrt(var + EPS)).astype(x.dtype) * g + b


def spatial_gating(u, v, w_s, b_s, ln_g, ln_b):
    bsz, s, _ = u.shape
    v = layer_norm(v, ln_g, ln_b)
    vc = v.reshape(bsz, s // CHUNK, CHUNK, N_HEADS_A, HEAD_DIM_A)
    mixed = jnp.einsum('hpq,bcqhd->bcphd', w_s, vc) + b_s.T[None, None, :, :, None]
    return u * mixed.reshape(bsz, s, D_A)


def fourier_mix(z, w_f, b_f):
    bsz, s, _ = z.shape
    zg = z.reshape(bsz, s, N_GROUPS_B, GROUP_DIM_B).astype(jnp.float32)
    f = jnp.fft.fftn(zg, axes=(1, 3), norm="ortho").real.astype(z.dtype)
    y = jnp.einsum('bsgc,gce->bsge', f, w_f) + b_f
    return y.reshape(bsz, s, D_B)


def setup_inputs(seed: int = 0) -> dict:
    key = jax.random.key(seed)
    ks = jax.random.split(key, 16)
    f32 = jnp.float32
    x = jax.random.normal(ks[0], (BATCH, SEQ, D_MODEL), f32)
    pre_g = 1.0 + 0.02 * jax.random.normal(ks[1], (DEPTH, D_MODEL), f32)
    post_g = 1.0 + 0.02 * jax.random.normal(ks[2], (DEPTH, D_MODEL), f32)
    w_in = jax.random.normal(ks[3], (DEPTH, D_MODEL, D_IN), f32) * D_MODEL ** -0.5
    ln_g = 1.0 + 0.02 * jax.random.normal(ks[4], (DEPTH, D_A), f32)
    ln_b = 0.02 * jax.random.normal(ks[5], (DEPTH, D_A), f32)
    w_s = jax.random.normal(ks[6], (DEPTH, N_HEADS_A, CHUNK, CHUNK), f32) * CHUNK ** -0.5
    b_s = 0.02 * jax.random.normal(ks[7], (DEPTH, N_HEADS_A, CHUNK), f32)
    w_f = jax.random.normal(ks[8], (DEPTH, N_GROUPS_B, GROUP_DIM_B, GROUP_DIM_B), f32) * GROUP_DIM_B ** -0.5
    b_f = 0.02 * jax.random.normal(ks[9], (DEPTH, N_GROUPS_B, GROUP_DIM_B), f32)
    w_out = jax.random.normal(ks[10], (DEPTH, D_MIX, D_MODEL), f32) * D_MIX ** -0.5
    return {"x": x, "pre_g": pre_g, "post_g": post_g, "w_in": w_in,
            "ln_g": ln_g, "ln_b": ln_b, "w_s": w_s, "b_s": b_s,
            "w_f": w_f, "b_f": b_f, "w_out": w_out}


def reference(x, pre_g, post_g, w_in, ln_g, ln_b, w_s, b_s, w_f, b_f, w_out):
    for l in range(DEPTH):
        h = rms_norm(x, pre_g[l])
        p = jnp.einsum('bsd,de->bse', h, w_in[l])
        u_a, v_a, g_a, z_b, g_b = jnp.split(
            p, [D_A, 2 * D_A, 3 * D_A, 3 * D_A + D_B], axis=-1)
        a = spatial_gating(jax.nn.gelu(u_a, approximate=False),
                           jax.nn.gelu(v_a, approximate=False),
                           w_s[l], b_s[l], ln_g[l], ln_b[l]) * jax.nn.silu(g_a)
        b = fourier_mix(z_b, w_f[l], b_f[l]) * jax.nn.silu(g_b)
        y = jnp.einsum('bse,ed->bsd', jnp.concatenate([a, b], axis=-1), w_out[l])
        x = x + rms_norm(y, post_g[l])
    return x
```

```python
import functools

import numpy as np

import jax
import jax.numpy as jnp
from jax import lax
from jax.experimental import pallas as pl
from jax.experimental.pallas import tpu as pltpu

N_HEADS_A = 4
CHUNK = 128
N_GROUPS_B = 4
EPS = 1e-6

F32 = jnp.float32
BF16 = jnp.bfloat16

MIX_ROWS = 512
OUT_ROWS = 512
FOLD_ROWS = 256
VMEM_LIMIT = 56 * 1024 * 1024


def _dft_cos_sin(n):
    k = np.arange(n, dtype=np.int64)
    ang = (2.0 * np.pi / n) * ((k[:, None] * k[None, :]) % n).astype(np.float64)
    return np.cos(ang), np.sin(ang)


def _gelu(x):
    return 0.5 * x * (1.0 + lax.erf(x * np.float32(1.0 / np.sqrt(2.0))))


def _silu(x):
    return x / (1.0 + jnp.exp(-x))


def _fold_body(w_in_ref, w_f_ref, cc_ref, sc_ref, out_ref, *, d_a, d_b, n_groups):
    gd = d_b // n_groups
    hi = lax.Precision.HIGHEST
    out_ref[:, 0:3 * d_a] = w_in_ref[:, 0:3 * d_a].astype(BF16)
    out_ref[:, 3 * d_a:3 * d_a + d_b] = w_in_ref[:, 3 * d_a + d_b:3 * d_a + 2 * d_b].astype(BF16)
    base = 3 * d_a + d_b
    for g in range(n_groups):
        wz = w_in_ref[:, 3 * d_a + g * gd:3 * d_a + (g + 1) * gd]
        mc = jnp.dot(cc_ref[...], w_f_ref[g], precision=hi, preferred_element_type=F32)
        ms = jnp.dot(sc_ref[...], w_f_ref[g], precision=hi, preferred_element_type=F32)
        out_ref[:, base + g * gd:base + (g + 1) * gd] = jnp.dot(
            wz, mc, precision=hi, preferred_element_type=F32).astype(BF16)
        out_ref[:, base + d_b + g * gd:base + d_b + (g + 1) * gd] = jnp.dot(
            wz, ms, precision=hi, preferred_element_type=F32).astype(BF16)


def _fold_weights(w_in, w_f, d_a, d_b):
    d_model, d_in = w_in.shape
    n_groups, gd, _ = w_f.shape
    cc, sc = _dft_cos_sin(gd)
    d_out = 3 * d_a + 3 * d_b
    return pl.pallas_call(
        functools.partial(_fold_body, d_a=d_a, d_b=d_b, n_groups=n_groups),
        grid=(d_model // FOLD_ROWS,),
        in_specs=[
            pl.BlockSpec((FOLD_ROWS, d_in), lambda i: (i, 0)),
            pl.BlockSpec((n_groups, gd, gd), lambda i: (0, 0, 0)),
            pl.BlockSpec((gd, gd), lambda i: (0, 0)),
            pl.BlockSpec((gd, gd), lambda i: (0, 0)),
        ],
        out_specs=pl.BlockSpec((FOLD_ROWS, d_out), lambda i: (i, 0)),
        out_shape=jax.ShapeDtypeStruct((d_model, d_out), BF16),
        name="fold_weights",
    )(w_in, w_f, jnp.asarray(cc, F32), jnp.asarray(sc, F32))


def _mix_body(x_ref, pre_g_ref, w_ref, ln_g_ref, ln_b_ref, w_s_ref, b_s_ref,
              a_ref, gate_ref, pp_ref, mixed_ref, *, d_a, d_b):
    rows = x_ref.shape[0]
    hd = d_a // N_HEADS_A

    x = x_ref[...]
    ms = jnp.mean(x * x, axis=-1, keepdims=True)
    h = ((x * lax.rsqrt(ms + EPS)) * pre_g_ref[...]).astype(BF16)

    def proj(lo, width):
        return jnp.dot(h, w_ref[:, lo:lo + width], preferred_element_type=F32)

    v = _gelu(proj(d_a, d_a))
    mu = jnp.mean(v, axis=-1, keepdims=True)
    dv = v - mu
    var = jnp.mean(dv * dv, axis=-1, keepdims=True)
    v_ln = ((dv * lax.rsqrt(var + EPS)) * ln_g_ref[...] + ln_b_ref[...]).astype(BF16)
    for c in range(rows // CHUNK):
        for n in range(N_HEADS_A):
            blk = v_ln[c * CHUNK:(c + 1) * CHUNK, n * hd:(n + 1) * hd]
            mixed_ref[c * CHUNK:(c + 1) * CHUNK, n * hd:(n + 1) * hd] = (
                jnp.dot(w_s_ref[n], blk, preferred_element_type=F32) + b_s_ref[:, n * hd:(n + 1) * hd])

    u = _gelu(proj(0, d_a))
    a_ref[...] = (u * mixed_ref[...] * _silu(proj(2 * d_a, d_a))).astype(BF16)
    gate_ref[...] = _silu(proj(3 * d_a, d_b)).astype(BF16)
    pp_ref[0, 0] = proj(3 * d_a + d_b, d_b).astype(BF16)
    pp_ref[0, 1] = proj(3 * d_a + 2 * d_b, d_b).astype(BF16)


def _mix(x2d, pre_g, w_fold, ln_g, ln_b, w_s, b_s_full, seq, d_a, d_b):
    m, d_model = x2d.shape
    tiles_per_seq = seq // MIX_ROWS
    d_out = w_fold.shape[1]
    const2 = lambda i: (0, 0)
    return pl.pallas_call(
        functools.partial(_mix_body, d_a=d_a, d_b=d_b),
        grid=(m // MIX_ROWS,),
        in_specs=[
            pl.BlockSpec((MIX_ROWS, d_model), lambda i: (i, 0)),
            pl.BlockSpec((1, d_model), const2),
            pl.BlockSpec((d_model, d_out), const2),
            pl.BlockSpec((1, d_a), const2),
            pl.BlockSpec((1, d_a), const2),
            pl.BlockSpec((N_HEADS_A, CHUNK, CHUNK), lambda i: (0, 0, 0)),
            pl.BlockSpec((CHUNK, d_a), const2),
        ],
        out_specs=[
            pl.BlockSpec((MIX_ROWS, d_a), lambda i: (i, 0)),
            pl.BlockSpec((MIX_ROWS, d_b), lambda i: (i, 0)),
            pl.BlockSpec((1, 2, MIX_ROWS, d_b),
                         lambda i: (i // tiles_per_seq, 0, i % tiles_per_seq, 0)),
        ],
        out_shape=[
            jax.ShapeDtypeStruct((m, d_a), BF16),
            jax.ShapeDtypeStruct((m, d_b), BF16),
            jax.ShapeDtypeStruct((m // seq, 2, seq, d_b), BF16),
        ],
        scratch_shapes=[pltpu.VMEM((MIX_ROWS, d_a), F32)],
        compiler_params=pltpu.CompilerParams(
            dimension_semantics=("arbitrary",), vmem_limit_bytes=VMEM_LIMIT),
        name="mix",
    )(x2d, pre_g, w_fold, ln_g, ln_b, w_s, b_s_full)


def _out_body(x_ref, a_ref, gate_ref, pp_ref, dft_ref, b_f_ref, w_out_ref, post_g_ref, o_ref, *, d_a):
    t = pl.program_id(1)
    rows = x_ref.shape[1]
    start = pl.multiple_of(t * rows, rows)
    f = jnp.dot(dft_ref[pl.ds(start, rows), :], pp_ref[0], preferred_element_type=F32)
    bg = ((f + b_f_ref[...]) * gate_ref[0].astype(F32)).astype(BF16)
    y = (jnp.dot(a_ref[0], w_out_ref[0:d_a, :], preferred_element_type=F32)
         + jnp.dot(bg, w_out_ref[d_a:, :], preferred_element_type=F32))
    ms = jnp.mean(y * y, axis=-1, keepdims=True)
    o_ref[0] = x_ref[0] + (y * lax.rsqrt(ms + EPS)) * post_g_ref[...]


def _out(x, a, gate, pp, dft, b_f, w_out, post_g, d_a, d_b):
    bsz, seq, d_model = x.shape
    const2 = lambda b, t: (0, 0)
    return pl.pallas_call(
        functools.partial(_out_body, d_a=d_a),
        grid=(bsz, seq // OUT_ROWS),
        in_specs=[
            pl.BlockSpec((1, OUT_ROWS, d_model), lambda b, t: (b, t, 0)),
            pl.BlockSpec((1, OUT_ROWS, d_a), lambda b, t: (b, t, 0)),
            pl.BlockSpec((1, OUT_ROWS, d_b), lambda b, t: (b, t, 0)),
            pl.BlockSpec((1, 2 * seq, d_b), lambda b, t: (b, 0, 0)),
            pl.BlockSpec((seq, 2 * seq), const2, pipeline_mode=pl.Buffered(1)),
            pl.BlockSpec((1, d_b), const2),
            pl.BlockSpec((d_a + d_b, d_model), const2, pipeline_mode=pl.Buffered(1)),
            pl.BlockSpec((1, d_model), const2),
        ],
        out_specs=pl.BlockSpec((1, OUT_ROWS, d_model), lambda b, t: (b, t, 0)),
        out_shape=jax.ShapeDtypeStruct((bsz, seq, d_model), F32),
        compiler_params=pltpu.CompilerParams(
            dimension_semantics=("arbitrary", "arbitrary"), vmem_limit_bytes=VMEM_LIMIT),
        name="out",
    )(x, a, gate, pp, dft, b_f, w_out, post_g)


def _seq_dft_matrix(seq, gd):
    c, s = _dft_cos_sin(seq)
    scale = 1.0 / np.sqrt(float(seq) * float(gd))
    mat = np.concatenate([c, -s], axis=1) * scale
    return jnp.asarray(mat, F32).astype(BF16)


def kernel(x, pre_g, post_g, w_in, ln_g, ln_b, w_s, b_s, w_f, b_f, w_out):
    bsz, seq, d_model = x.shape
    depth = w_in.shape[0]
    d_a = ln_g.shape[1]
    d_b = w_out.shape[1] - d_a
    gd = w_f.shape[2]
    hd = d_a // N_HEADS_A
    dft = _seq_dft_matrix(seq, gd)
    for l in range(depth):
        w_fold = _fold_weights(w_in[l], w_f[l], d_a, d_b)
        b_s_full = jnp.repeat(b_s[l].T, hd, axis=1)
        a, gate, pp = _mix(x.reshape(bsz * seq, d_model), pre_g[l][None], w_fold, ln_g[l][None],
                           ln_b[l][None], w_s[l].astype(BF16), b_s_full, seq, d_a, d_b)
        x = _out(x, a.reshape(bsz, seq, d_a), gate.reshape(bsz, seq, d_b),
                 pp.reshape(bsz, 2 * seq, d_b), dft, b_f[l].reshape(1, d_b), w_out[l].astype(BF16),
                 post_g[l][None], d_a, d_b)
    return x
```

```python
import functools

import numpy as np

import jax
import jax.numpy as jnp
from jax import lax
from jax.experimental import pallas as pl
from jax.experimental.pallas import tpu as pltpu

N_HEADS_A = 4
CHUNK = 128
N_GROUPS_B = 4
EPS = 1e-6

F32 = jnp.float32
BF16 = jnp.bfloat16

SUBLANES = 8
PACK_ROWS = 16
MIX_ROWS = 512
OUT_ROWS = 512
FOLD_ROWS = 256
VMEM_LIMIT = 56 * 1024 * 1024


def _dft_cos_sin(n, rows=None, cols=None):
    k = np.arange(n if rows is None else rows, dtype=np.int64)
    m = np.arange(n if cols is None else cols, dtype=np.int64)
    ang = (2.0 * np.pi / n) * ((k[:, None] * m[None, :]) % n).astype(np.float64)
    return np.cos(ang), np.sin(ang)


def _gelu(x):
    return 0.5 * x * (1.0 + lax.erf(x * np.float32(1.0 / np.sqrt(2.0))))


def _silu(x):
    return x / (1.0 + jnp.exp(-x))


def _mirror_group(src_ref, g, width):
    r = lax.broadcasted_iota(jnp.int32, (SUBLANES, width), 0)
    lo = src_ref[pl.ds(pl.multiple_of(g * SUBLANES, SUBLANES), SUBLANES), :]
    hi = src_ref[pl.ds(pl.multiple_of((g + 1) * SUBLANES, SUBLANES), SUBLANES), :]
    return jnp.where(r == 0, hi, jnp.take_along_axis(lo, (SUBLANES - r) & (SUBLANES - 1), axis=0))


def _mirror_rows16(src_ref, n_groups, i, width):
    g = n_groups - 1 - 2 * i
    return jnp.concatenate([_mirror_group(src_ref, g, width), _mirror_group(src_ref, g - 1, width)], axis=0)


def _fold_body(w_in_ref, w_f_ref, cc_ref, sc_ref, out_ref, *, d_a, d_b, n_groups):
    gd = d_b // n_groups
    hi = lax.Precision.HIGHEST
    out_ref[:, 0:3 * d_a] = w_in_ref[:, 0:3 * d_a].astype(BF16)
    out_ref[:, 3 * d_a:3 * d_a + d_b] = w_in_ref[:, 3 * d_a + d_b:3 * d_a + 2 * d_b].astype(BF16)
    base = 3 * d_a + d_b
    for g in range(n_groups):
        wz = w_in_ref[:, 3 * d_a + g * gd:3 * d_a + (g + 1) * gd]
        mc = jnp.dot(cc_ref[...], w_f_ref[g], precision=hi, preferred_element_type=F32)
        ms = jnp.dot(sc_ref[...], w_f_ref[g], precision=hi, preferred_element_type=F32)
        out_ref[:, base + g * gd:base + (g + 1) * gd] = jnp.dot(
            wz, mc, precision=hi, preferred_element_type=F32).astype(BF16)
        out_ref[:, base + d_b + g * gd:base + d_b + (g + 1) * gd] = jnp.dot(
            wz, ms, precision=hi, preferred_element_type=F32).astype(BF16)


def _fold_weights(w_in, w_f, d_a, d_b):
    d_model, d_in = w_in.shape
    n_groups, gd, _ = w_f.shape
    cc, sc = _dft_cos_sin(gd)
    d_out = 3 * d_a + 3 * d_b
    return pl.pallas_call(
        functools.partial(_fold_body, d_a=d_a, d_b=d_b, n_groups=n_groups),
        grid=(d_model // FOLD_ROWS,),
        in_specs=[
            pl.BlockSpec((FOLD_ROWS, d_in), lambda i: (i, 0)),
            pl.BlockSpec((n_groups, gd, gd), lambda i: (0, 0, 0)),
            pl.BlockSpec((gd, gd), lambda i: (0, 0)),
            pl.BlockSpec((gd, gd), lambda i: (0, 0)),
        ],
        out_specs=pl.BlockSpec((FOLD_ROWS, d_out), lambda i: (i, 0)),
        out_shape=jax.ShapeDtypeStruct((d_model, d_out), BF16),
        name="fold_weights",
    )(w_in, w_f, jnp.asarray(cc, F32), jnp.asarray(sc, F32))


def _mix_body(x_ref, pre_g_ref, w_ref, ln_g_ref, ln_b_ref, w_s_ref, b_s_ref,
              a_ref, gate_ref, eo_ref, nyq_ref, mixed_ref, z_ref, *, d_a, d_b, seq):
    t = pl.program_id(1)
    rows = x_ref.shape[1]
    hd = d_a // N_HEADS_A
    half = seq // 2

    x = x_ref[0]
    ms = jnp.mean(x * x, axis=-1, keepdims=True)
    h = ((x * lax.rsqrt(ms + EPS)) * pre_g_ref[...]).astype(BF16)

    def proj(lo, width):
        return jnp.dot(h, w_ref[:, lo:lo + width], preferred_element_type=F32)

    v = _gelu(proj(d_a, d_a))
    mu = jnp.mean(v, axis=-1, keepdims=True)
    dv = v - mu
    var = jnp.mean(dv * dv, axis=-1, keepdims=True)
    v_ln = ((dv * lax.rsqrt(var + EPS)) * ln_g_ref[...] + ln_b_ref[...]).astype(BF16)
    for c in range(rows // CHUNK):
        for n in range(N_HEADS_A):
            blk = v_ln[c * CHUNK:(c + 1) * CHUNK, n * hd:(n + 1) * hd]
            mixed_ref[c * CHUNK:(c + 1) * CHUNK, n * hd:(n + 1) * hd] = (
                jnp.dot(w_s_ref[n], blk, preferred_element_type=F32) + b_s_ref[:, n * hd:(n + 1) * hd])

    u = _gelu(proj(0, d_a))
    a_ref[0] = (u * mixed_ref[...] * _silu(proj(2 * d_a, d_a))).astype(BF16)
    gate_ref[0] = _silu(proj(3 * d_a, d_b)).astype(BF16)
    z_ref[pl.ds(pl.multiple_of(t * rows, rows), rows), :] = proj(3 * d_a + d_b, 2 * d_b)

    @pl.when(t == pl.num_programs(1) - 1)
    def _():
        z_ref[seq:seq + SUBLANES, :] = z_ref[half:half + SUBLANES, :]
        nyq_ref[0] = z_ref[half:half + SUBLANES, 0:d_b]

        def fold(i, carry):
            top = z_ref[pl.ds(pl.multiple_of(i * PACK_ROWS, PACK_ROWS), PACK_ROWS), :]
            mir = _mirror_rows16(z_ref, seq // SUBLANES, i, 2 * d_b)
            dst = pl.ds(pl.multiple_of(i * PACK_ROWS, PACK_ROWS), PACK_ROWS)
            eo_ref[0, 0, dst, :] = (top[:, :d_b] + mir[:, :d_b]).astype(BF16)
            eo_ref[0, 1, dst, :] = (top[:, d_b:] - mir[:, d_b:]).astype(BF16)
            return carry

        lax.fori_loop(0, half // PACK_ROWS, fold, 0)


def _mix(x, pre_g, w_fold, ln_g, ln_b, w_s, b_s_full, d_a, d_b):
    bsz, seq, d_model = x.shape
    d_out = w_fold.shape[1]
    const2 = lambda b, t: (0, 0)
    return pl.pallas_call(
        functools.partial(_mix_body, d_a=d_a, d_b=d_b, seq=seq),
        grid=(bsz, seq // MIX_ROWS),
        in_specs=[
            pl.BlockSpec((1, MIX_ROWS, d_model), lambda b, t: (b, t, 0)),
            pl.BlockSpec((1, d_model), const2),
            pl.BlockSpec((d_model, d_out), const2, pipeline_mode=pl.Buffered(1)),
            pl.BlockSpec((1, d_a), const2),
            pl.BlockSpec((1, d_a), const2),
            pl.BlockSpec((N_HEADS_A, CHUNK, CHUNK), lambda b, t: (0, 0, 0)),
            pl.BlockSpec((CHUNK, d_a), const2),
        ],
        out_specs=[
            pl.BlockSpec((1, MIX_ROWS, d_a), lambda b, t: (b, t, 0)),
            pl.BlockSpec((1, MIX_ROWS, d_b), lambda b, t: (b, t, 0)),
            pl.BlockSpec((1, 2, seq // 2, d_b), lambda b, t: (b, 0, 0, 0)),
            pl.BlockSpec((1, SUBLANES, d_b), lambda b, t: (b, 0, 0)),
        ],
        out_shape=[
            jax.ShapeDtypeStruct((bsz, seq, d_a), BF16),
            jax.ShapeDtypeStruct((bsz, seq, d_b), BF16),
            jax.ShapeDtypeStruct((bsz, 2, seq // 2, d_b), BF16),
            jax.ShapeDtypeStruct((bsz, SUBLANES, d_b), F32),
        ],
        scratch_shapes=[pltpu.VMEM((MIX_ROWS, d_a), F32),
                        pltpu.VMEM((seq + SUBLANES, 2 * d_b), F32)],
        compiler_params=pltpu.CompilerParams(
            dimension_semantics=("arbitrary", "arbitrary"), vmem_limit_bytes=VMEM_LIMIT),
        name="mix",
    )(x, pre_g, w_fold, ln_g, ln_b, w_s, b_s_full)


def _out_body(x_ref, a_ref, gate_ref, eo_ref, nyq_ref, cmat_ref, smat_ref, alt_ref, b_f_ref, w_out_ref,
              post_g_ref, o_ref, lo_ref, r_ref, bg_ref, *, d_a, seq):
    t = pl.program_id(1)
    rows = x_ref.shape[1]
    half = seq // 2
    d_b = bg_ref.shape[1]

    @pl.when(t == 0)
    def _():
        e = eo_ref[0, 0]
        p = jnp.dot(cmat_ref[0:half, :], e, preferred_element_type=F32)
        q = jnp.dot(smat_ref[...], eo_ref[0, 1], preferred_element_type=F32)
        lo_ref[...] = p - q
        r_ref[0:half, :] = p + q
        r_ref[half:half + SUBLANES, :] = jnp.dot(cmat_ref[half:half + SUBLANES, :], e, preferred_element_type=F32)
        bias8 = b_f_ref[...] + alt_ref[...] * nyq_ref[0, 0:1, :]
        bias = jnp.concatenate([bias8, bias8], axis=0)

        def gate(i, carry):
            src = pl.ds(pl.multiple_of(i * PACK_ROWS, PACK_ROWS), PACK_ROWS)
            bg_ref[src, :] = ((lo_ref[src, :] + bias) * gate_ref[0, src, :].astype(F32)).astype(BF16)
            dst = pl.ds(pl.multiple_of(half + i * PACK_ROWS, PACK_ROWS), PACK_ROWS)
            mir = _mirror_rows16(r_ref, half // SUBLANES, i, d_b)
            bg_ref[dst, :] = ((mir + bias) * gate_ref[0, dst, :].astype(F32)).astype(BF16)
            return carry

        lax.fori_loop(0, half // PACK_ROWS, gate, 0)

    bg = bg_ref[pl.ds(pl.multiple_of(t * rows, rows), rows), :]
    y = (jnp.dot(a_ref[0], w_out_ref[0:d_a, :], preferred_element_type=F32)
         + jnp.dot(bg, w_out_ref[d_a:, :], preferred_element_type=F32))
    ms = jnp.mean(y * y, axis=-1, keepdims=True)
    o_ref[0] = x_ref[0] + (y * lax.rsqrt(ms + EPS)) * post_g_ref[...]


def _out(x, a, gate, eo, nyq, cmat, smat, alt, b_f, w_out, post_g, d_a, d_b):
    bsz, seq, d_model = x.shape
    half = seq // 2
    const2 = lambda b, t: (0, 0)
    return pl.pallas_call(
        functools.partial(_out_body, d_a=d_a, seq=seq),
        grid=(bsz, seq // OUT_ROWS),
        in_specs=[
            pl.BlockSpec((1, OUT_ROWS, d_model), lambda b, t: (b, t, 0)),
            pl.BlockSpec((1, OUT_ROWS, d_a), lambda b, t: (b, t, 0)),
            pl.BlockSpec((1, seq, d_b), lambda b, t: (b, 0, 0)),
            pl.BlockSpec((1, 2, half, d_b), lambda b, t: (b, 0, 0, 0)),
            pl.BlockSpec((1, SUBLANES, d_b), lambda b, t: (b, 0, 0)),
            pl.BlockSpec((half + SUBLANES, half), const2, pipeline_mode=pl.Buffered(1)),
            pl.BlockSpec((half, half), const2, pipeline_mode=pl.Buffered(1)),
            pl.BlockSpec((SUBLANES, d_b), const2),
            pl.BlockSpec((1, d_b), const2),
            pl.BlockSpec((d_a + d_b, d_model), const2, pipeline_mode=pl.Buffered(1)),
            pl.BlockSpec((1, d_model), const2),
        ],
        out_specs=pl.BlockSpec((1, OUT_ROWS, d_model), lambda b, t: (b, t, 0)),
        out_shape=jax.ShapeDtypeStruct((bsz, seq, d_model), F32),
        scratch_shapes=[pltpu.VMEM((half, d_b), F32),
                        pltpu.VMEM((half + SUBLANES, d_b), F32),
                        pltpu.VMEM((seq, d_b), BF16)],
        compiler_params=pltpu.CompilerParams(
            dimension_semantics=("arbitrary", "arbitrary"), vmem_limit_bytes=VMEM_LIMIT),
        name="out",
    )(x, a, gate, eo, nyq, cmat, smat, alt, b_f, w_out, post_g)


def _seq_dft_constants(seq, gd, d_b):
    half = seq // 2
    scale = 1.0 / np.sqrt(float(seq) * float(gd))
    c, s = _dft_cos_sin(seq, rows=half + SUBLANES, cols=half)
    c[half + 1:] = 0.0
    alt = np.zeros((SUBLANES, d_b))
    alt[1::2] = -2.0 * scale
    return (jnp.asarray(c * scale, F32).astype(BF16), jnp.asarray(s[:half] * scale, F32).astype(BF16),
            jnp.asarray(alt, F32))


def kernel(x, pre_g, post_g, w_in, ln_g, ln_b, w_s, b_s, w_f, b_f, w_out):
    bsz, seq, d_model = x.shape
    depth = w_in.shape[0]
    d_a = ln_g.shape[1]
    d_b = w_out.shape[1] - d_a
    gd = w_f.shape[2]
    hd = d_a // N_HEADS_A
    cmat, smat, alt = _seq_dft_constants(seq, gd, d_b)
    for l in range(depth):
        w_fold = _fold_weights(w_in[l], w_f[l], d_a, d_b)
        b_s_full = jnp.repeat(b_s[l].T, hd, axis=1)
        a, gate, eo, nyq = _mix(x, pre_g[l][None], w_fold, ln_g[l][None], ln_b[l][None],
                                w_s[l].astype(BF16), b_s_full, d_a, d_b)
        x = _out(x, a, gate, eo, nyq, cmat, smat, alt, b_f[l].reshape(1, d_b), w_out[l].astype(BF16),
                 post_g[l][None], d_a, d_b)
    return x
```

```python
import functools

import numpy as np

import jax
import jax.numpy as jnp
from jax import lax
from jax.experimental import pallas as pl
from jax.experimental.pallas import tpu as pltpu

N_HEADS_A = 4
CHUNK = 128
N_GROUPS_B = 4
EPS = 1e-6

F32 = jnp.float32
BF16 = jnp.bfloat16

SUBLANES = 8
PACK_ROWS = 16
STEP_ROWS = 512
FOLD_ROWS = 256
VMEM_LIMIT = 56 * 1024 * 1024


def _dft_cos_sin(n, rows=None, cols=None):
    k = np.arange(n if rows is None else rows, dtype=np.int64)
    m = np.arange(n if cols is None else cols, dtype=np.int64)
    ang = (2.0 * np.pi / n) * ((k[:, None] * m[None, :]) % n).astype(np.float64)
    return np.cos(ang), np.sin(ang)


def _gelu(x):
    return 0.5 * x * (1.0 + lax.erf(x * np.float32(1.0 / np.sqrt(2.0))))


def _silu(x):
    return x / (1.0 + jnp.exp(-x))


def _mirror_group(src_ref, g, width):
    r = lax.broadcasted_iota(jnp.int32, (SUBLANES, width), 0)
    lo = src_ref[pl.ds(pl.multiple_of(g * SUBLANES, SUBLANES), SUBLANES), :]
    hi = src_ref[pl.ds(pl.multiple_of((g + 1) * SUBLANES, SUBLANES), SUBLANES), :]
    return jnp.where(r == 0, hi, jnp.take_along_axis(lo, (SUBLANES - r) & (SUBLANES - 1), axis=0))


def _mirror_rows16(src_ref, n_groups, i, width):
    g = n_groups - 1 - 2 * i
    return jnp.concatenate([_mirror_group(src_ref, g, width), _mirror_group(src_ref, g - 1, width)], axis=0)


def _fold_body(w_in_ref, w_f_ref, cc_ref, sc_ref, out_ref, *, d_a, d_b, n_groups):
    gd = d_b // n_groups
    hi = lax.Precision.HIGHEST
    out_ref[:, 0:3 * d_a] = w_in_ref[:, 0:3 * d_a].astype(BF16)
    out_ref[:, 3 * d_a:3 * d_a + d_b] = w_in_ref[:, 3 * d_a + d_b:3 * d_a + 2 * d_b].astype(BF16)
    base = 3 * d_a + d_b
    for g in range(n_groups):
        wz = w_in_ref[:, 3 * d_a + g * gd:3 * d_a + (g + 1) * gd]
        mc = jnp.dot(cc_ref[...], w_f_ref[g], precision=hi, preferred_element_type=F32)
        ms = jnp.dot(sc_ref[...], w_f_ref[g], precision=hi, preferred_element_type=F32)
        out_ref[:, base + g * gd:base + (g + 1) * gd] = jnp.dot(
            wz, mc, precision=hi, preferred_element_type=F32).astype(BF16)
        out_ref[:, base + d_b + g * gd:base + d_b + (g + 1) * gd] = jnp.dot(
            wz, ms, precision=hi, preferred_element_type=F32).astype(BF16)


def _fold_weights(w_in, w_f, d_a, d_b):
    d_model, d_in = w_in.shape
    n_groups, gd, _ = w_f.shape
    cc, sc = _dft_cos_sin(gd)
    d_out = 3 * d_a + 3 * d_b
    return pl.pallas_call(
        functools.partial(_fold_body, d_a=d_a, d_b=d_b, n_groups=n_groups),
        grid=(d_model // FOLD_ROWS,),
        in_specs=[
            pl.BlockSpec((FOLD_ROWS, d_in), lambda i: (i, 0)),
            pl.BlockSpec((n_groups, gd, gd), lambda i: (0, 0, 0)),
            pl.BlockSpec((gd, gd), lambda i: (0, 0)),
            pl.BlockSpec((gd, gd), lambda i: (0, 0)),
        ],
        out_specs=pl.BlockSpec((FOLD_ROWS, d_out), lambda i: (i, 0)),
        out_shape=jax.ShapeDtypeStruct((d_model, d_out), BF16),
        name="fold_weights",
    )(w_in, w_f, jnp.asarray(cc, F32), jnp.asarray(sc, F32))


def _mix_tile(x, row0, pre_g_ref, w_ref, ln_g_ref, ln_b_ref, w_s_ref, b_s_ref,
              mixed_ref, z_ref, a_ref, gate_ref, *, d_a, d_b):
    rows = x.shape[0]
    hd = d_a // N_HEADS_A
    dst = pl.ds(pl.multiple_of(row0, rows), rows)

    ms = jnp.mean(x * x, axis=-1, keepdims=True)
    h = ((x * lax.rsqrt(ms + EPS)) * pre_g_ref[...]).astype(BF16)

    def proj(lo, width):
        return jnp.dot(h, w_ref[:, lo:lo + width], preferred_element_type=F32)

    v = _gelu(proj(d_a, d_a))
    mu = jnp.mean(v, axis=-1, keepdims=True)
    dv = v - mu
    var = jnp.mean(dv * dv, axis=-1, keepdims=True)
    v_ln = ((dv * lax.rsqrt(var + EPS)) * ln_g_ref[...] + ln_b_ref[...]).astype(BF16)
    for c in range(rows // CHUNK):
        for n in range(N_HEADS_A):
            blk = v_ln[c * CHUNK:(c + 1) * CHUNK, n * hd:(n + 1) * hd]
            mixed_ref[c * CHUNK:(c + 1) * CHUNK, n * hd:(n + 1) * hd] = (
                jnp.dot(w_s_ref[n], blk, preferred_element_type=F32) + b_s_ref[:, n * hd:(n + 1) * hd])

    u = _gelu(proj(0, d_a))
    a_ref[dst, :] = (u * mixed_ref[...] * _silu(proj(2 * d_a, d_a))).astype(BF16)
    gate_ref[dst, :] = _silu(proj(3 * d_a, d_b)).astype(BF16)
    z_ref[dst, :] = proj(3 * d_a + d_b, 2 * d_b)


def _fourier_phase(z_ref, gate_ref, cmat_ref, smat_ref, alt_ref, b_f_ref, eo_ref, lo_ref, r_ref, bg_ref,
                   *, seq, d_b):
    half = seq // 2
    z_ref[seq:seq + SUBLANES, :] = z_ref[half:half + SUBLANES, :]

    def fold(i, carry):
        dst = pl.ds(pl.multiple_of(i * PACK_ROWS, PACK_ROWS), PACK_ROWS)
        top = z_ref[dst, :]
        mir = _mirror_rows16(z_ref, seq // SUBLANES, i, 2 * d_b)
        eo_ref[0, dst, :] = (top[:, :d_b] + mir[:, :d_b]).astype(BF16)
        eo_ref[1, dst, :] = (top[:, d_b:] - mir[:, d_b:]).astype(BF16)
        return carry

    lax.fori_loop(0, half // PACK_ROWS, fold, 0)

    e = eo_ref[0]
    p = jnp.dot(cmat_ref[0:half, :], e, preferred_element_type=F32)
    q = jnp.dot(smat_ref[...], eo_ref[1], preferred_element_type=F32)
    lo_ref[...] = p - q
    r_ref[0:half, :] = p + q
    r_ref[half:half + SUBLANES, :] = jnp.dot(cmat_ref[half:half + SUBLANES, :], e, preferred_element_type=F32)
    bias8 = b_f_ref[...] + alt_ref[...] * z_ref[half:half + 1, 0:d_b]
    bias = jnp.concatenate([bias8, bias8], axis=0)

    def gate(i, carry):
        src = pl.ds(pl.multiple_of(i * PACK_ROWS, PACK_ROWS), PACK_ROWS)
        bg_ref[src, :] = ((lo_ref[src, :] + bias) * gate_ref[src, :].astype(F32)).astype(BF16)
        dst = pl.ds(pl.multiple_of(half + i * PACK_ROWS, PACK_ROWS), PACK_ROWS)
        mir = _mirror_rows16(r_ref, half // SUBLANES, i, d_b)
        bg_ref[dst, :] = ((mir + bias) * gate_ref[dst, :].astype(F32)).astype(BF16)
        return carry

    lax.fori_loop(0, half // PACK_ROWS, gate, 0)


def _layer_body(x_ref, pre_g_ref, w_ref, ln_g_ref, ln_b_ref, w_s_ref, b_s_ref, cmat_ref, smat_ref, alt_ref,
                b_f_ref, w_out_ref, post_g_ref, o_ref,
                mixed_ref, z_ref, a_ref, gate_ref, eo_ref, lo_ref, r_ref, bg_ref, *, d_a, d_b, seq):
    t = pl.program_id(1)
    n_tiles = pl.num_programs(1) // 2
    rows = x_ref.shape[1]

    @pl.when(t < n_tiles)
    def _():
        _mix_tile(x_ref[0], t * rows, pre_g_ref, w_ref, ln_g_ref, ln_b_ref, w_s_ref, b_s_ref,
                  mixed_ref, z_ref, a_ref, gate_ref, d_a=d_a, d_b=d_b)

    @pl.when(t == n_tiles - 1)
    def _():
        _fourier_phase(z_ref, gate_ref, cmat_ref, smat_ref, alt_ref, b_f_ref, eo_ref, lo_ref, r_ref, bg_ref,
                       seq=seq, d_b=d_b)

    @pl.when(t >= n_tiles)
    def _():
        src = pl.ds(pl.multiple_of((t - n_tiles) * rows, rows), rows)
        y = (jnp.dot(a_ref[src, :], w_out_ref[0:d_a, :], preferred_element_type=F32)
             + jnp.dot(bg_ref[src, :], w_out_ref[d_a:, :], preferred_element_type=F32))
        ms = jnp.mean(y * y, axis=-1, keepdims=True)
        o_ref[0] = x_ref[0] + (y * lax.rsqrt(ms + EPS)) * post_g_ref[...]


def _layer(x, pre_g, w_fold, ln_g, ln_b, w_s, b_s_full, cmat, smat, alt, b_f, w_out, post_g, d_a, d_b):
    bsz, seq, d_model = x.shape
    half = seq // 2
    n_tiles = seq // STEP_ROWS
    d_out = w_fold.shape[1]
    const2 = lambda b, t: (0, 0)
    resident = dict(pipeline_mode=pl.Buffered(1))
    return pl.pallas_call(
        functools.partial(_layer_body, d_a=d_a, d_b=d_b, seq=seq),
        grid=(bsz, 2 * n_tiles),
        in_specs=[
            pl.BlockSpec((1, STEP_ROWS, d_model), lambda b, t: (b, t % n_tiles, 0)),
            pl.BlockSpec((1, d_model), const2),
            pl.BlockSpec((d_model, d_out), const2, **resident),
            pl.BlockSpec((1, d_a), const2),
            pl.BlockSpec((1, d_a), const2),
            pl.BlockSpec((N_HEADS_A, CHUNK, CHUNK), lambda b, t: (0, 0, 0)),
            pl.BlockSpec((CHUNK, d_a), const2),
            pl.BlockSpec((half + SUBLANES, half), const2, **resident),
            pl.BlockSpec((half, half), const2, **resident),
            pl.BlockSpec((SUBLANES, d_b), const2),
            pl.BlockSpec((1, d_b), const2),
            pl.BlockSpec((d_a + d_b, d_model), const2, **resident),
            pl.BlockSpec((1, d_model), const2),
        ],
        out_specs=pl.BlockSpec((1, STEP_ROWS, d_model), lambda b, t: (b, jnp.maximum(t - n_tiles, 0), 0)),
        out_shape=jax.ShapeDtypeStruct((bsz, seq, d_model), F32),
        scratch_shapes=[
            pltpu.VMEM((STEP_ROWS, d_a), F32),
            pltpu.VMEM((seq + SUBLANES, 2 * d_b), F32),
            pltpu.VMEM((seq, d_a), BF16),
            pltpu.VMEM((seq, d_b), BF16),
            pltpu.VMEM((2, half, d_b), BF16),
            pltpu.VMEM((half, d_b), F32),
            pltpu.VMEM((half + SUBLANES, d_b), F32),
            pltpu.VMEM((seq, d_b), BF16),
        ],
        compiler_params=pltpu.CompilerParams(
            dimension_semantics=("arbitrary", "arbitrary"), vmem_limit_bytes=VMEM_LIMIT),
        name="layer",
    )(x, pre_g, w_fold, ln_g, ln_b, w_s, b_s_full, cmat, smat, alt, b_f, w_out, post_g)


def _seq_dft_constants(seq, gd, d_b):
    half = seq // 2
    scale = 1.0 / np.sqrt(float(seq) * float(gd))
    c, s = _dft_cos_sin(seq, rows=half + SUBLANES, cols=half)
    c[half + 1:] = 0.0
    alt = np.zeros((SUBLANES, d_b))
    alt[1::2] = -2.0 * scale
    return (jnp.asarray(c * scale, F32).astype(BF16), jnp.asarray(s[:half] * scale, F32).astype(BF16),
            jnp.asarray(alt, F32))


def kernel(x, pre_g, post_g, w_in, ln_g, ln_b, w_s, b_s, w_f, b_f, w_out):
    bsz, seq, d_model = x.shape
    depth = w_in.shape[0]
    d_a = ln_g.shape[1]
    d_b = w_out.shape[1] - d_a
    gd = w_f.shape[2]
    hd = d_a // N_HEADS_A
    cmat, smat, alt = _seq_dft_constants(seq, gd, d_b)
    for l in range(depth):
        w_fold = _fold_weights(w_in[l], w_f[l], d_a, d_b)
        b_s_full = jnp.repeat(b_s[l].T, hd, axis=1)
        x = _layer(x, pre_g[l][None], w_fold, ln_g[l][None], ln_b[l][None], w_s[l].astype(BF16), b_s_full,
                   cmat, smat, alt, b_f[l].reshape(1, d_b), w_out[l].astype(BF16), post_g[l][None], d_a, d_b)
    return x
```
